```python
import math
import jax, jax.numpy as jnp
from jax import lax
import numpy as np

D_MODEL = 1024
BATCH = 4
SEQ = 8192
DEPTH = 1

ATTN_HEADS = 8
ATTN_KV_HEADS = 2
ATTN_HEAD_DIM = 64
WINDOW = 128
ATTN_BLOCK = 128
ROPE_THETA = 10000.0
DN_HEADS = 4
DN_HEAD_DIM = 128
DN_CONV = 4
DN_CHUNK = 64
ATTN_Q = ATTN_HEADS * ATTN_HEAD_DIM
ATTN_KV = ATTN_KV_HEADS * ATTN_HEAD_DIM
DN_DIM = DN_HEADS * DN_HEAD_DIM
MIX_WIDTH = ATTN_Q + DN_DIM
SPLITS = (ATTN_Q, ATTN_KV, ATTN_KV, DN_DIM, DN_DIM, DN_DIM, DN_HEADS, DN_HEADS, DN_DIM)
IN_WIDTH = ATTN_Q + 2 * ATTN_KV + 4 * DN_DIM + 2 * DN_HEADS
PEER_HEADS = 8
PEER_N_KEYS = 128
PEER_N_EXPERTS = PEER_N_KEYS * PEER_N_KEYS
PEER_QUERY_DIM = 256
PEER_TOPK = 16
PEER_TOKEN_BLOCK = 128
DEEPNORM_ALPHA = (2.0 * DEPTH) ** 0.25
DEEPNORM_BETA = (8.0 * DEPTH) ** -0.25
LN_EPS = 1e-5
RMS_EPS = 1e-6
L2_EPS = 1e-6
NEG_INF = -1e30

kernel_name = 'hymba_swa_sink_gdn_peer_deepnorm'


def layer_norm(x, g, b):
    xf = x.astype(jnp.float32)
    mu = jnp.mean(xf, axis=-1, keepdims=True)
    var = jnp.mean(jnp.square(xf - mu), axis=-1, keepdims=True)
    return ((xf - mu) * lax.rsqrt(var + LN_EPS) * g.astype(jnp.float32) + b.astype(jnp.float32)).astype(x.dtype)


def rope(x, positions):
    half = x.shape[-1] // 2
    inv_freq = ROPE_THETA ** (-jnp.arange(half, dtype=jnp.float32) / half)
    ang = positions.astype(jnp.float32)[..., None] * inv_freq
    cos = jnp.cos(ang)[:, :, None, :]
    sin = jnp.sin(ang)[:, :, None, :]
    xf = x.astype(jnp.float32)
    x1, x2 = xf[..., :half], xf[..., half:]
    return jnp.concatenate([x1 * cos - x2 * sin, x2 * cos + x1 * sin], axis=-1).astype(x.dtype)


def sliding_window_sink_attention(q, k, v, sinks):
    B, T, _, hd = q.shape
    L = ATTN_BLOCK
    nb = T // L
    G = ATTN_HEADS // ATTN_KV_HEADS
    qb = q.reshape(B, nb, L, ATTN_KV_HEADS, G, hd)

    def with_prev(a):
        ab = a.reshape(B, nb, L, ATTN_KV_HEADS, hd)
        prev = jnp.pad(ab[:, :-1], ((0, 0), (1, 0), (0, 0), (0, 0), (0, 0)))
        return jnp.concatenate([prev, ab], axis=2)

    kb, vb = with_prev(k), with_prev(v)
    s = jnp.einsum('bnqhgd,bnkhd->bnhgqk', qb, kb, preferred_element_type=jnp.float32) * (hd ** -0.5)
    blk = jnp.arange(nb)[:, None] * L
    qpos = blk + jnp.arange(L)[None, :]
    kpos = blk - L + jnp.arange(2 * L)[None, :]
    delta = qpos[:, :, None] - kpos[:, None, :]
    allowed = (delta >= 0) & (delta < WINDOW) & (kpos[:, None, :] >= 0)
    s = jnp.where(allowed[None, :, None, None], s, NEG_INF)
    sink_col = jnp.broadcast_to(
        sinks.astype(jnp.float32).reshape(1, 1, ATTN_KV_HEADS, G, 1, 1), s.shape[:-1] + (1,))
    p = jax.nn.softmax(jnp.concatenate([s, sink_col], axis=-1), axis=-1)[..., :-1]
    o = jnp.einsum('bnhgqk,bnkhd->bnqhgd', p.astype(v.dtype), vb)
    return o.reshape(B, T, ATTN_HEADS * hd)


def causal_short_conv(x, w):
    K, C = w.shape
    y = lax.conv_general_dilated(
        x, w[:, None, :].astype(x.dtype), window_strides=(1,), padding=[(K - 1, 0)],
        dimension_numbers=('NWC', 'WIO', 'NWC'), feature_group_count=C)
    return jax.nn.silu(y)


def gated_delta_rule(q, k, v, g, beta):
    B, H, T, dk = q.shape
    dv = v.shape[-1]
    C = DN_CHUNK
    N = T // C
    q = q * (dk ** -0.5)
    q, k, v = (a.reshape(B, H, N, C, a.shape[-1]) for a in (q, k, v))
    g, beta = (a.reshape(B, H, N, C) for a in (g, beta))
    gc = jnp.cumsum(g, axis=-1)
    causal = jnp.tril(jnp.ones((C, C), dtype=bool))
    strict = jnp.tril(jnp.ones((C, C), dtype=bool), -1)
    diff = gc[..., :, None] - gc[..., None, :]
    decay = jnp.where(causal, jnp.exp(jnp.where(causal, diff, 0.0)), 0.0)
    k_beta = k * beta[..., None]
    a_mat = jnp.where(strict, jnp.einsum('bhnid,bhnjd->bhnij', k_beta, k) * decay, 0.0)
    eye = jnp.eye(C, dtype=q.dtype)
    rhs = jnp.concatenate([v * beta[..., None], k_beta * jnp.exp(gc)[..., None]], axis=-1)
    sol = lax.linalg.triangular_solve(eye + a_mat, rhs, left_side=True, lower=True, unit_diagonal=True)
    u, w = sol[..., :dv], sol[..., dv:]
    qk = jnp.einsum('bhnid,bhnjd->bhnij', q, k) * decay
    q_dec = q * jnp.exp(gc)[..., None]
    k_dec = k * jnp.exp(gc[..., -1:] - gc)[..., None]
    g_last = jnp.exp(gc[..., -1])

    def step(S, xs):
        qk_i, qd_i, kd_i, u_i, w_i, gl_i = xs
        v_new = u_i - jnp.einsum('bhid,bhde->bhie', w_i, S)
        o_i = jnp.einsum('bhid,bhde->bhie', qd_i, S) + jnp.einsum('bhij,bhje->bhie', qk_i, v_new)
        S = S * gl_i[..., None, None] + jnp.einsum('bhid,bhie->bhde', kd_i, v_new)
        return S, o_i

    xs = tuple(jnp.moveaxis(a, 2, 0) for a in (qk, q_dec, k_dec, u, w, g_last))
    S0 = jnp.zeros((B, H, dk, dv), q.dtype)
    _, o = lax.scan(step, S0, xs)
    return jnp.moveaxis(o, 0, 2).reshape(B, H, T, dv)


def l2_normalize(a):
    return a * lax.rsqrt(jnp.sum(jnp.square(a), axis=-1, keepdims=True) + L2_EPS)


def hybrid_mixer(x, positions, w_in, attn_sinks, conv_w, a_log, dt_bias, dn_norm_g, w_out):
    B, T, _ = x.shape
    proj = x @ w_in
    offsets = np.cumsum(SPLITS)[:-1].tolist()
    q_a, k_a, v_a, q_d, k_d, v_d, b_d, a_d, z_d = jnp.split(proj, offsets, axis=-1)

    q_a = rope(q_a.reshape(B, T, ATTN_HEADS, ATTN_HEAD_DIM), positions)
    k_a = rope(k_a.reshape(B, T, ATTN_KV_HEADS, ATTN_HEAD_DIM), positions)
    v_a = v_a.reshape(B, T, ATTN_KV_HEADS, ATTN_HEAD_DIM)
    o_attn = sliding_window_sink_attention(q_a, k_a, v_a, attn_sinks)

    qkv = causal_short_conv(jnp.concatenate([q_d, k_d, v_d], axis=-1), conv_w).astype(jnp.float32)
    q_d, k_d, v_d = jnp.split(qkv, 3, axis=-1)
    to_heads = lambda a: jnp.transpose(a.reshape(B, T, DN_HEADS, DN_HEAD_DIM), (0, 2, 1, 3))
    q_d, k_d, v_d = l2_normalize(to_heads(q_d)), l2_normalize(to_heads(k_d)), to_heads(v_d)
    beta = jnp.transpose(jax.nn.sigmoid(b_d.astype(jnp.float32)), (0, 2, 1))
    g = -jnp.exp(a_log.astype(jnp.float32)) * jax.nn.softplus(
        a_d.astype(jnp.float32) + dt_bias.astype(jnp.float32))
    g = jnp.transpose(g, (0, 2, 1))
    o_dn = jnp.transpose(gated_delta_rule(q_d, k_d, v_d, g, beta), (0, 2, 1, 3))
    o_dn = o_dn * lax.rsqrt(jnp.mean(jnp.square(o_dn), axis=-1, keepdims=True) + RMS_EPS)
    o_dn = o_dn * dn_norm_g.astype(jnp.float32) * jax.nn.silu(
        z_d.reshape(B, T, DN_HEADS, DN_HEAD_DIM).astype(jnp.float32))
    o_dn = o_dn.reshape(B, T, DN_DIM).astype(x.dtype)

    return jnp.concatenate([o_attn, o_dn], axis=-1) @ w_out


def peer_ffn(x, w_query, sub_keys, expert_u, expert_v):
    B, T, D = x.shape
    L = PEER_TOKEN_BLOCK
    K = PEER_TOPK
    xt = x.reshape(-1, L, D)

    def one_block(xb):
        q = (xb @ w_query).reshape(L, PEER_HEADS, 2, PEER_QUERY_DIM // 2)
        s = jnp.einsum('lhpd,hpkd->lhpk', q, sub_keys, preferred_element_type=jnp.float32)
        vals, idx = lax.top_k(s, K)
        cand = (vals[..., 0, :, None] + vals[..., 1, None, :]).reshape(L, PEER_HEADS, K * K)
        cidx = (idx[..., 0, :, None] * PEER_N_KEYS + idx[..., 1, None, :]).reshape(L, PEER_HEADS, K * K)
        best, pos = lax.top_k(cand, K)
        expert = jnp.take_along_axis(cidx, pos, axis=-1)
        gate = jax.nn.softmax(best, axis=-1)
        u = expert_u[expert]
        v = expert_v[expert]
        h = jax.nn.gelu(jnp.einsum('ld,lhkd->lhk', xb, u), approximate=False)
        return jnp.einsum('lhk,lhkd->ld', (h * gate).astype(xb.dtype), v)

    return lax.map(one_block, xt).reshape(B, T, D)


def setup_inputs(seed: int = 0) -> dict:
    key = jax.random.key(seed)
    ks = jax.random.split(key, 20)
    f32 = jnp.float32
    nrm = lambda k, shape, std: jax.random.normal(k, shape, f32) * std
    x = nrm(ks[0], (BATCH, SEQ, D_MODEL), 1.0)
    positions = jnp.broadcast_to(jnp.arange(SEQ, dtype=jnp.int32)[None, :], (BATCH, SEQ))
    w_in = nrm(ks[1], (D_MODEL, IN_WIDTH), D_MODEL ** -0.5)
    attn_sinks = nrm(ks[2], (ATTN_HEADS,), 0.5)
    conv_w = nrm(ks[3], (DN_CONV, 3 * DN_DIM), DN_CONV ** -0.5)
    a_log = jnp.log(jax.random.uniform(ks[4], (DN_HEADS,), f32, 1.0, 16.0))
    dt = jnp.exp(jax.random.uniform(ks[5], (DN_HEADS,), f32, math.log(1e-3), math.log(1e-1)))
    dt_bias = dt + jnp.log(-jnp.expm1(-dt))
    dn_norm_g = 1.0 + nrm(ks[6], (DN_HEAD_DIM,), 0.02)
    w_out = nrm(ks[7], (MIX_WIDTH, D_MODEL), MIX_WIDTH ** -0.5 * DEEPNORM_BETA)
    ln1_g = 1.0 + nrm(ks[8], (D_MODEL,), 0.02)
    ln1_b = nrm(ks[9], (D_MODEL,), 0.02)
    peer_w_query = nrm(ks[10], (D_MODEL, PEER_HEADS * PEER_QUERY_DIM), D_MODEL ** -0.5)
    peer_sub_keys = nrm(ks[11], (PEER_HEADS, 2, PEER_N_KEYS, PEER_QUERY_DIM // 2), (PEER_QUERY_DIM // 2) ** -0.5)
    peer_u = nrm(ks[12], (PEER_N_EXPERTS, D_MODEL), D_MODEL ** -0.5)
    peer_v = nrm(ks[13], (PEER_N_EXPERTS, D_MODEL), DEEPNORM_BETA)
    ln2_g = 1.0 + nrm(ks[14], (D_MODEL,), 0.02)
    ln2_b = nrm(ks[15], (D_MODEL,), 0.02)
    return {'x': x, 'positions': positions, 'w_in': w_in, 'attn_sinks': attn_sinks,
            'conv_w': conv_w, 'a_log': a_log, 'dt_bias': dt_bias, 'dn_norm_g': dn_norm_g,
            'w_out': w_out, 'ln1_g': ln1_g, 'ln1_b': ln1_b, 'peer_w_query': peer_w_query,
            'peer_sub_keys': peer_sub_keys, 'peer_u': peer_u, 'peer_v': peer_v,
            'ln2_g': ln2_g, 'ln2_b': ln2_b}


def reference(x, positions, w_in, attn_sinks, conv_w, a_log, dt_bias, dn_norm_g, w_out,
              ln1_g, ln1_b, peer_w_query, peer_sub_keys, peer_u, peer_v, ln2_g, ln2_b):
    h = x
    for _ in range(DEPTH):
        mix = hybrid_mixer(h, positions, w_in, attn_sinks, conv_w, a_log, dt_bias, dn_norm_g, w_out)
        h = layer_norm(DEEPNORM_ALPHA * h + mix, ln1_g, ln1_b)
        ffn = peer_ffn(h, peer_w_query, peer_sub_keys, peer_u, peer_v)
        h = layer_norm(DEEPNORM_ALPHA * h + ffn, ln2_g, ln2_b)
    return h
```

```python
import functools
import math

import jax
import jax.numpy as jnp
from jax import lax
from jax.experimental import pallas as pl
from jax.experimental.pallas import tpu as pltpu

F32 = jnp.float32
BF16 = jnp.bfloat16
HIGHEST = lax.Precision.HIGHEST

D_MODEL = 1024
ATTN_HEADS = 8
ATTN_KV_HEADS = 2
ATTN_GROUP = ATTN_HEADS // ATTN_KV_HEADS
ATTN_HEAD_DIM = 64
ATTN_BLOCK = 128
ROPE_THETA = 10000.0
DN_HEADS = 4
DN_HEAD_DIM = 128
DN_CONV = 4
DN_CHUNK = 64
ATTN_Q = ATTN_HEADS * ATTN_HEAD_DIM
ATTN_KV = ATTN_KV_HEADS * ATTN_HEAD_DIM
DN_DIM = DN_HEADS * DN_HEAD_DIM
PEER_HEADS = 8
PEER_N_KEYS = 128
PEER_N_EXPERTS = PEER_N_KEYS * PEER_N_KEYS
PEER_HALF_QUERY = 128
PEER_TOPK = 16
PEER_SLOTS = PEER_HEADS * PEER_TOPK
DEEPNORM_ALPHA = 2.0 ** 0.25
LN_EPS = 1e-5
RMS_EPS = 1e-6
L2_EPS = 1e-6
NEG_INF = -1e30

LANES = 128
SUBLANES = 8
ATTN_W = ATTN_Q + 2 * ATTN_KV
DN_QKV_W = 3 * DN_DIM
BA_W = LANES
PACK_ROWS = D_MODEL // (2 * LANES)

VMEM_LIMIT = 48 * 1024 * 1024
VMEM_LIMIT_TABLE = 56 * 1024 * 1024


def _params(sem, limit=VMEM_LIMIT):
    return pltpu.CompilerParams(dimension_semantics=sem, vmem_limit_bytes=limit)


def _dot(a, b, precision=None):
    return jnp.dot(a, b, preferred_element_type=F32, precision=precision)


def _dot_nt(a, b, precision=None):
    return lax.dot_general(a, b, (((1,), (1,)), ((), ())), preferred_element_type=F32, precision=precision)


def _proj_kernel(x_ref, w_ref, qkv_a_ref, qkv_d_ref, z_ref, ba_ref):
    xb = x_ref[...].astype(BF16)
    o = 0
    for ref in (qkv_a_ref, qkv_d_ref, z_ref, ba_ref):
        w = ref.shape[1]
        ref[...] = _dot(xb, w_ref[:, o:o + w])
        o += w


def _proj(x2, w_r, tm=256):
    n = x2.shape[0]
    widths = (ATTN_W, DN_QKV_W, DN_DIM, BA_W)
    return pl.pallas_call(
        _proj_kernel,
        grid=(n // tm,),
        in_specs=[pl.BlockSpec((tm, D_MODEL), lambda i: (i, 0)),
                  pl.BlockSpec((D_MODEL, sum(widths)), lambda i: (0, 0))],
        out_specs=[pl.BlockSpec((tm, w), lambda i: (i, 0)) for w in widths],
        out_shape=[jax.ShapeDtypeStruct((n, w), F32) for w in widths],
        compiler_params=_params(("parallel",)),
        name="proj",
    )(x2, w_r)


def _rope(x, cos, sin_signed, first_half):
    outs = []
    for c in range(x.shape[1] // LANES):
        xc = x[:, c * LANES:(c + 1) * LANES]
        fwd = pltpu.roll(xc, LANES - ATTN_HEAD_DIM // 2, axis=1)
        bwd = pltpu.roll(xc, ATTN_HEAD_DIM // 2, axis=1)
        outs.append(xc * cos + jnp.where(first_half, fwd, bwd) * sin_signed)
    return outs[0] if len(outs) == 1 else jnp.concatenate(outs, axis=1)


def _attn_kernel(sinks_ref, invf_ref, pos_ref, posp_ref, cur_ref, prev_ref, o_ref):
    n = pl.program_id(1)
    L = ATTN_BLOCK
    hd = ATTN_HEAD_DIM
    lane = lax.broadcasted_iota(jnp.int32, (L, LANES), 1)
    first_half = (lane % hd) < (hd // 2)

    def tables(p_ref):
        ang = p_ref[...].astype(F32) * invf_ref[...]
        return jnp.cos(ang), jnp.where(first_half, -jnp.sin(ang), jnp.sin(ang))

    cos_c, sin_c = tables(pos_ref)
    cos_p, sin_p = tables(posp_ref)
    q = _rope(cur_ref[:, 0:ATTN_Q], cos_c, sin_c, first_half) * (hd ** -0.5)
    k_cur = _rope(cur_ref[:, ATTN_Q:ATTN_Q + ATTN_KV], cos_c, sin_c, first_half)
    k_prev = _rope(prev_ref[:, 0:ATTN_KV], cos_p, sin_p, first_half)
    v_cur = cur_ref[:, ATTN_Q + ATTN_KV:ATTN_W]
    v_prev = prev_ref[:, ATTN_KV:2 * ATTN_KV]

    rows = ATTN_GROUP * L
    qi = lax.broadcasted_iota(jnp.int32, (rows, 2 * L), 0) % L
    kj = lax.broadcasted_iota(jnp.int32, (rows, 2 * L), 1)
    delta = qi + L - kj
    allowed = (delta >= 0) & (delta < L) & ((kj >= L) | (n > 0))
    row_g = lax.broadcasted_iota(jnp.int32, (rows, 1), 0) // L

    outs = []
    for h in range(ATTN_KV_HEADS):
        ks = slice(h * hd, (h + 1) * hd)
        kcat = jnp.concatenate([k_prev[:, ks], k_cur[:, ks]], axis=0).astype(BF16)
        vcat = jnp.concatenate([v_prev[:, ks], v_cur[:, ks]], axis=0).astype(BF16)
        qh = jnp.concatenate(
            [q[:, (h * ATTN_GROUP + g) * hd:(h * ATTN_GROUP + g + 1) * hd] for g in range(ATTN_GROUP)],
            axis=0).astype(BF16)
        s = jnp.where(allowed, _dot_nt(qh, kcat), NEG_INF)
        sink = jnp.zeros((rows, 1), F32)
        for g in range(ATTN_GROUP):
            sink = jnp.where(row_g == g, sinks_ref[h * ATTN_GROUP + g], sink)
        m = jnp.maximum(jnp.max(s, axis=1, keepdims=True), sink)
        e = jnp.exp(s - m)
        den = jnp.sum(e, axis=1, keepdims=True) + jnp.exp(sink - m)
        p = (e / den).astype(BF16)
        oh = _dot(p, vcat)
        outs.extend(oh[g * L:(g + 1) * L] for g in range(ATTN_GROUP))
    o_ref[...] = jnp.concatenate(outs, axis=1).astype(o_ref.dtype)


def _attention(qkv_a, pos_col, invf, sinks, nb_per_seq):
    n = qkv_a.shape[0]
    L = ATTN_BLOCK
    nb = nb_per_seq
    batch = n // (nb * L)
    cur = lambda b, i: (b * nb + i, 0)
    prev = lambda b, i: (b * nb + jnp.maximum(i - 1, 0), 0)
    prev_kv = lambda b, i: (b * nb + jnp.maximum(i - 1, 0), ATTN_Q // (2 * ATTN_KV))
    return pl.pallas_call(
        _attn_kernel,
        grid=(batch, nb),
        in_specs=[pl.BlockSpec(memory_space=pltpu.SMEM),
                  pl.BlockSpec((1, LANES), lambda b, i: (0, 0)),
                  pl.BlockSpec((L, 1), cur),
                  pl.BlockSpec((L, 1), prev),
                  pl.BlockSpec((L, ATTN_W), cur),
                  pl.BlockSpec((L, 2 * ATTN_KV), prev_kv)],
        out_specs=pl.BlockSpec((L, ATTN_Q), cur),
        out_shape=jax.ShapeDtypeStruct((n, ATTN_Q), BF16),
        compiler_params=_params(("parallel", "parallel")),
        name="attn",
    )(sinks, invf, pos_col, pos_col, qkv_a, qkv_a)


def _unit_lower_inverse(a):
    c = a.shape[0]
    eye = (lax.broadcasted_iota(jnp.int32, (c, c), 0) == lax.broadcasted_iota(jnp.int32, (c, c), 1)).astype(F32)
    t = eye - a
    p = _dot(a, a, HIGHEST)
    steps = int(math.log2(c)) - 1
    for s in range(steps):
        t = t + _dot(t, p, HIGHEST)
        if s + 1 < steps:
            p = _dot(p, p, HIGHEST)
    return t


def _dn_kernel(qkv_ref, ba_ref, z_ref, convw_ref, alog_ref, dtb_ref, gnorm_ref, o_ref, xbuf, s_ref, *, ct):
    C = DN_CHUNK
    dk = DN_HEAD_DIM
    halo = SUBLANES

    @pl.when(pl.program_id(1) == 0)
    def _():
        xbuf[0:halo, :] = jnp.zeros((halo, DN_QKV_W), F32)
        s_ref[...] = jnp.zeros(s_ref.shape, F32)

    xbuf[halo:halo + ct, :] = qkv_ref[...]
    acc = xbuf[halo:halo + ct, :] * convw_ref[DN_CONV - 1:DN_CONV, :]
    for j in range(DN_CONV - 1):
        off = halo - (DN_CONV - 1) + j
        acc = acc + xbuf[off:off + ct, :] * convw_ref[j:j + 1, :]
    xbuf[0:halo, :] = xbuf[ct:ct + halo, :]
    qkv = acc * jax.nn.sigmoid(acc)

    ba = ba_ref[...]
    beta_all = jax.nn.sigmoid(ba)
    sp_in = ba + dtb_ref[...]
    softplus = jnp.maximum(sp_in, 0.0) + jnp.log1p(jnp.exp(-jnp.abs(sp_in)))
    g_all = -jnp.exp(alog_ref[...]) * softplus

    ri = lax.broadcasted_iota(jnp.int32, (C, C), 0)
    ci = lax.broadcasted_iota(jnp.int32, (C, C), 1)
    causal = ri >= ci
    strict = ri > ci
    ltri = causal.astype(F32)

    for j in range(ct // C):
        r = slice(j * C, (j + 1) * C)
        gc_all = _dot(ltri, g_all[r], HIGHEST)
        gc_t = gc_all.T
        outs = []
        for h in range(DN_HEADS):
            hs = slice(h * dk, (h + 1) * dk)
            gc = gc_all[:, DN_HEADS + h:DN_HEADS + h + 1]
            gcr = gc_t[DN_HEADS + h:DN_HEADS + h + 1, :]
            gl = gc[C - 1:C, :]
            beta = beta_all[r, h:h + 1]
            qh = qkv[r, h * dk:(h + 1) * dk]
            kh = qkv[r, DN_DIM + h * dk:DN_DIM + (h + 1) * dk]
            vh = qkv[r, 2 * DN_DIM + h * dk:2 * DN_DIM + (h + 1) * dk]
            qh = qh * lax.rsqrt(jnp.sum(qh * qh, axis=1, keepdims=True) + L2_EPS) * (dk ** -0.5)
            kh = kh * lax.rsqrt(jnp.sum(kh * kh, axis=1, keepdims=True) + L2_EPS)
            decay = jnp.where(causal, jnp.exp(jnp.where(causal, gc - gcr, 0.0)), 0.0)
            egc = jnp.exp(gc)
            kb = kh * beta
            a_mat = jnp.where(strict, _dot_nt(kb, kh, HIGHEST) * decay, 0.0)
            t_inv = _unit_lower_inverse(a_mat)
            u = _dot(t_inv, vh * beta, HIGHEST)
            w = _dot(t_inv, kb * egc, HIGHEST)
            kh_b = kh.astype(BF16)
            qk = _dot_nt(qh.astype(BF16), kh_b) * decay
            q_dec = (qh * egc).astype(BF16)
            k_dec = kh * jnp.exp(gl - gc)
            s = s_ref[h]
            s_b = s.astype(BF16)
            v_new = u - _dot(w.astype(BF16), s_b)
            o = _dot(q_dec, s_b) + _dot(qk.astype(BF16), v_new.astype(BF16))
            s_ref[h] = s * jnp.exp(gl) + _dot(k_dec.T.astype(BF16), v_new.astype(BF16))
            o = o * lax.rsqrt(jnp.mean(o * o, axis=1, keepdims=True) + RMS_EPS)
            zh = z_ref[r, hs]
            outs.append(o * gnorm_ref[...] * (zh * jax.nn.sigmoid(zh)))
        o_ref[r, :] = jnp.concatenate(outs, axis=1).astype(o_ref.dtype)


def _deltanet(qkv_d, ba, z, conv_w, alog_vec, dtb_vec, gnorm, seq, ct=256):
    n = qkv_d.shape[0]
    ct = min(ct, seq)
    nc = seq // ct
    batch = n // seq
    tok = lambda b, c: (b * nc + c, 0)
    const = lambda b, c: (0, 0)
    return pl.pallas_call(
        functools.partial(_dn_kernel, ct=ct),
        grid=(batch, nc),
        in_specs=[pl.BlockSpec((ct, DN_QKV_W), tok),
                  pl.BlockSpec((ct, BA_W), tok),
                  pl.BlockSpec((ct, DN_DIM), tok),
                  pl.BlockSpec((DN_CONV, DN_QKV_W), const),
                  pl.BlockSpec((1, BA_W), const),
                  pl.BlockSpec((1, BA_W), const),
                  pl.BlockSpec((1, DN_HEAD_DIM), const)],
        out_specs=pl.BlockSpec((ct, DN_DIM), tok),
        out_shape=jax.ShapeDtypeStruct((n, DN_DIM), BF16),
        scratch_shapes=[pltpu.VMEM((ct + SUBLANES, DN_QKV_W), F32),
                        pltpu.VMEM((DN_HEADS, DN_HEAD_DIM, DN_HEAD_DIM), F32)],
        compiler_params=_params(("parallel", "arbitrary")),
        name="deltanet",
    )(qkv_d, ba, z, conv_w, alog_vec, dtb_vec, gnorm)


def _layer_norm(y, g, b):
    mu = jnp.mean(y, axis=1, keepdims=True)
    yc = y - mu
    var = jnp.mean(yc * yc, axis=1, keepdims=True)
    return yc * lax.rsqrt(var + LN_EPS) * g + b


def _mix_ln_kernel(x_ref, oa_ref, od_ref, wa_ref, wd_ref, g_ref, b_ref, o_ref):
    mix = _dot(oa_ref[...], wa_ref[...]) + _dot(od_ref[...], wd_ref[...])
    o_ref[...] = _layer_norm(DEEPNORM_ALPHA * x_ref[...] + mix, g_ref[...], b_ref[...])


def _mix_ln(x2, o_attn, o_dn, w_a, w_d, g, b, tm=256):
    n = x2.shape[0]
    tok = lambda i: (i, 0)
    const = lambda i: (0, 0)
    return pl.pallas_call(
        _mix_ln_kernel,
        grid=(n // tm,),
        in_specs=[pl.BlockSpec((tm, D_MODEL), tok),
                  pl.BlockSpec((tm, ATTN_Q), tok),
                  pl.BlockSpec((tm, DN_DIM), tok),
                  pl.BlockSpec((ATTN_Q, D_MODEL), const),
                  pl.BlockSpec((DN_DIM, D_MODEL), const),
                  pl.BlockSpec((1, D_MODEL), const),
                  pl.BlockSpec((1, D_MODEL), const)],
        out_specs=pl.BlockSpec((tm, D_MODEL), tok),
        out_shape=jax.ShapeDtypeStruct((n, D_MODEL), F32),
        compiler_params=_params(("parallel",)),
        name="mix_ln",
    )(x2, o_attn, o_dn, w_a, w_d, g, b)


def _top_rows(s, k, payload=None):
    r = s.shape[0]
    rid = lax.broadcasted_iota(jnp.int32, s.shape, 0).astype(F32)
    vals, ids = [], []
    for _ in range(k):
        m = jnp.max(s, axis=0, keepdims=True)
        am = jnp.min(jnp.where(s == m, rid, float(r)), axis=0, keepdims=True)
        hit = rid == am
        vals.append(m)
        ids.append(am if payload is None else jnp.max(jnp.where(hit, payload, -1.0), axis=0, keepdims=True))
        s = jnp.where(hit, -jnp.inf, s)
    return jnp.concatenate(vals, axis=0), jnp.concatenate(ids, axis=0)


def _route_kernel(x_ref, wq_ref, keys_ref, idx_ref, gate_ref):
    xb = x_ref[...].astype(BF16)
    K = PEER_TOPK
    experts, gates = [], []
    for h in range(PEER_HEADS):
        vals, ids = [], []
        for p in range(2):
            col = (h * 2 + p) * PEER_HALF_QUERY
            q = _dot(xb, wq_ref[:, col:col + PEER_HALF_QUERY]).astype(BF16)
            s_t = _dot_nt(keys_ref[h * 2 + p], q)
            v, i = _top_rows(s_t, K)
            vals.append(v)
            ids.append(i)
        cand = jnp.concatenate([vals[0][i:i + 1] + vals[1] for i in range(K)], axis=0)
        cidx = jnp.concatenate([ids[0][i:i + 1] * float(PEER_N_KEYS) + ids[1] for i in range(K)], axis=0)
        best, expert = _top_rows(cand, K, payload=cidx)
        e = jnp.exp(best - best[0:1])
        gates.append(e / jnp.sum(e, axis=0, keepdims=True))
        experts.append(expert)
    idx_ref[...] = jnp.concatenate(experts, axis=0).T.astype(jnp.int32)
    gate_ref[...] = jnp.concatenate(gates, axis=0).T


def _route(x1, wq, keys, tb=128):
    n = x1.shape[0]
    tok = lambda i: (i, 0)
    return pl.pallas_call(
        _route_kernel,
        grid=(n // tb,),
        in_specs=[pl.BlockSpec((tb, D_MODEL), tok),
                  pl.BlockSpec(wq.shape, lambda i: (0, 0)),
                  pl.BlockSpec(keys.shape, lambda i: (0, 0, 0))],
        out_specs=[pl.BlockSpec((tb, PEER_SLOTS), tok), pl.BlockSpec((tb, PEER_SLOTS), tok)],
        out_shape=[jax.ShapeDtypeStruct((n, PEER_SLOTS), jnp.int32),
                   jax.ShapeDtypeStruct((n, PEER_SLOTS), F32)],
        compiler_params=_params(("parallel",)),
        name="route",
    )(x1, wq, keys)


def _pack_table(tbl):
    e = tbl.shape[0]
    bits = lax.bitcast_convert_type(tbl.astype(BF16), jnp.uint16).astype(jnp.uint32)
    bits = bits.reshape(e, 2, PACK_ROWS, LANES)
    words = (bits[:, 1] << 16) | bits[:, 0]
    return lax.bitcast_convert_type(words, jnp.int32).reshape(e * PACK_ROWS, LANES)


def _unpack(words):
    lo = pltpu.bitcast(words << 16, F32)
    hi = pltpu.bitcast(words & jnp.int32(-65536), F32)
    return lo, hi


def _peer_u_kernel(idx_ref, x_ref, gate_ref, tbl_ref, coef_ref, pbuf, hbuf, *, tb):
    ones = jnp.ones((LANES, LANES), F32)
    eye = (lax.broadcasted_iota(jnp.int32, (PEER_SLOTS, LANES), 0)
           == lax.broadcasted_iota(jnp.int32, (PEER_SLOTS, LANES), 1)).astype(F32)

    def token(t, carry):
        xt = x_ref[t]
        xlo, xhi = xt[0:PACK_ROWS], xt[PACK_ROWS:2 * PACK_ROWS]
        for k in range(PEER_SLOTS):
            row = pl.multiple_of(idx_ref[t, k] * PACK_ROWS, PACK_ROWS)
            lo, hi = _unpack(tbl_ref[pl.ds(row, PACK_ROWS), :])
            pbuf[k * PACK_ROWS:(k + 1) * PACK_ROWS, :] = lo * xlo + hi * xhi
        q = pbuf[pl.ds(0, PEER_SLOTS, stride=PACK_ROWS), :]
        for s in range(1, PACK_ROWS):
            q = q + pbuf[pl.ds(s, PEER_SLOTS, stride=PACK_ROWS), :]
        rs = _dot(q, ones, HIGHEST)
        hbuf[pl.ds(t, 1), :] = jnp.sum(rs * eye, axis=0, keepdims=True)
        return carry

    lax.fori_loop(0, tb, token, 0)
    h = hbuf[...]
    gelu = 0.5 * h * (1.0 + lax.erf(h * (2.0 ** -0.5)))
    coef_ref[...] = gelu * gate_ref[...]


def _peer_u(idx, x1_tiles, gate, tbl, tb=128):
    n = idx.shape[0]
    return pl.pallas_call(
        functools.partial(_peer_u_kernel, tb=tb),
        grid=(n // tb,),
        in_specs=[pl.BlockSpec((tb, PEER_SLOTS), lambda i: (i, 0), memory_space=pltpu.SMEM),
                  pl.BlockSpec((tb, SUBLANES, LANES), lambda i: (i, 0, 0)),
                  pl.BlockSpec((tb, PEER_SLOTS), lambda i: (i, 0)),
                  pl.BlockSpec(memory_space=pltpu.VMEM)],
        out_specs=pl.BlockSpec((tb, PEER_SLOTS), lambda i: (i, 0)),
        out_shape=jax.ShapeDtypeStruct((n, PEER_SLOTS), F32),
        scratch_shapes=[pltpu.VMEM((PEER_SLOTS * PACK_ROWS, LANES), F32),
                        pltpu.VMEM((tb, PEER_SLOTS), F32)],
        compiler_params=_params(("parallel",), VMEM_LIMIT_TABLE),
        name="peer_u",
    )(idx, x1_tiles, gate, tbl)


def _peer_v_kernel(idx_ref, coef_ref, tbl_ref, o_ref, *, tb):
    n_acc = 4

    def token(t, carry):
        acc_lo = [jnp.zeros((PACK_ROWS, LANES), F32) for _ in range(n_acc)]
        acc_hi = [jnp.zeros((PACK_ROWS, LANES), F32) for _ in range(n_acc)]
        for k in range(PEER_SLOTS):
            row = pl.multiple_of(idx_ref[t, k] * PACK_ROWS, PACK_ROWS)
            lo, hi = _unpack(tbl_ref[pl.ds(row, PACK_ROWS), :])
            c = coef_ref[t, k]
            acc_lo[k % n_acc] = acc_lo[k % n_acc] + c * lo
            acc_hi[k % n_acc] = acc_hi[k % n_acc] + c * hi
        lo = (acc_lo[0] + acc_lo[1]) + (acc_lo[2] + acc_lo[3])
        hi = (acc_hi[0] + acc_hi[1]) + (acc_hi[2] + acc_hi[3])
        o_ref[t] = jnp.concatenate([lo, hi], axis=0)
        return carry

    lax.fori_loop(0, tb, token, 0)


def _peer_v(idx, coef, tbl, tb=128):
    n = idx.shape[0]
    smem = lambda: pl.BlockSpec((tb, PEER_SLOTS), lambda i: (i, 0), memory_space=pltpu.SMEM)
    return pl.pallas_call(
        functools.partial(_peer_v_kernel, tb=tb),
        grid=(n // tb,),
        in_specs=[smem(), smem(), pl.BlockSpec(memory_space=pltpu.VMEM)],
        out_specs=pl.BlockSpec((tb, SUBLANES, LANES), lambda i: (i, 0, 0)),
        out_shape=jax.ShapeDtypeStruct((n, SUBLANES, LANES), F32),
        compiler_params=_params(("parallel",), VMEM_LIMIT_TABLE),
        name="peer_v",
    )(idx, coef, tbl)


def _ln2_kernel(x_ref, f_ref, g_ref, b_ref, o_ref):
    o_ref[...] = _layer_norm(DEEPNORM_ALPHA * x_ref[...] + f_ref[...], g_ref[...], b_ref[...])


def _ln2(x1, ffn, g, b, tm=512):
    n = x1.shape[0]
    tm = min(tm, n)
    tok = lambda i: (i, 0)
    const = lambda i: (0, 0)
    return pl.pallas_call(
        _ln2_kernel,
        grid=(n // tm,),
        in_specs=[pl.BlockSpec((tm, D_MODEL), tok), pl.BlockSpec((tm, D_MODEL), tok),
                  pl.BlockSpec((1, D_MODEL), const), pl.BlockSpec((1, D_MODEL), const)],
        out_specs=pl.BlockSpec((tm, D_MODEL), tok),
        out_shape=jax.ShapeDtypeStruct((n, D_MODEL), F32),
        compiler_params=_params(("parallel",)),
        name="ln2",
    )(x1, ffn, g, b)


def _regroup_w_in(w_in):
    o_b = ATTN_W + DN_QKV_W
    o_z = o_b + 2 * DN_HEADS
    ba = jnp.pad(w_in[:, o_b:o_z], ((0, 0), (0, BA_W - 2 * DN_HEADS)))
    return jnp.concatenate([w_in[:, :o_b], w_in[:, o_z:], ba], axis=1).astype(BF16)


def _mixer(x2, pos_col, seq, w_in, attn_sinks, conv_w, a_log, dt_bias, dn_norm_g, w_out, ln1_g, ln1_b):
    qkv_a, qkv_d, z, ba = _proj(x2, _regroup_w_in(w_in))
    half = ATTN_HEAD_DIM // 2
    inv_freq = ROPE_THETA ** (-jnp.arange(half, dtype=F32) / half)
    invf = jnp.tile(inv_freq, LANES // half).reshape(1, LANES)
    o_attn = _attention(qkv_a, pos_col, invf, attn_sinks.astype(F32), seq // ATTN_BLOCK)
    head_lanes = lambda v: jnp.pad(v.astype(F32), (DN_HEADS, BA_W - 2 * DN_HEADS)).reshape(1, BA_W)
    o_dn = _deltanet(qkv_d, ba, z, conv_w.astype(F32), head_lanes(a_log), head_lanes(dt_bias),
                     dn_norm_g.astype(F32).reshape(1, DN_HEAD_DIM), seq)
    w_out_b = w_out.astype(BF16)
    return _mix_ln(x2, o_attn, o_dn, w_out_b[:ATTN_Q], w_out_b[ATTN_Q:],
                   ln1_g.reshape(1, D_MODEL), ln1_b.reshape(1, D_MODEL))


def _peer(x1, peer_w_query, peer_sub_keys, peer_u, peer_v, ln2_g, ln2_b):
    n = x1.shape[0]
    keys = peer_sub_keys.astype(BF16).reshape(PEER_HEADS * 2, PEER_N_KEYS, PEER_HALF_QUERY)
    idx, gate = _route(x1, peer_w_query.astype(BF16), keys)
    coef = _peer_u(idx, x1.reshape(n, SUBLANES, LANES), gate, _pack_table(peer_u))
    ffn = _peer_v(idx, coef, _pack_table(peer_v)).reshape(n, D_MODEL)
    return _ln2(x1, ffn, ln2_g.reshape(1, D_MODEL), ln2_b.reshape(1, D_MODEL))


def kernel(x, positions, w_in, attn_sinks, conv_w, a_log, dt_bias, dn_norm_g, w_out, ln1_g, ln1_b,
           peer_w_query, peer_sub_keys, peer_u, peer_v, ln2_g, ln2_b):
    b, t, d = x.shape
    x2 = x.reshape(b * t, d)
    pos_col = positions.reshape(b * t, 1)
    x1 = _mixer(x2, pos_col, t, w_in, attn_sinks, conv_w, a_log, dt_bias, dn_norm_g, w_out, ln1_g, ln1_b)
    out = _peer(x1, peer_w_query, peer_sub_keys, peer_u, peer_v, ln2_g, ln2_b)
    return out.reshape(b, t, d)
```
